```python
import jax, jax.numpy as jnp
from jax import lax
import numpy as np

D_MODEL = 4096
BATCH = 2
SEQ = 8192
DEPTH = 4
DEC_BATCH = 8
DEC_SEQ = 64
PAST_LEN = 2048

CHUNK = 64
N_MIXERS = 2
N_MLA = (DEPTH + 1) // 2
N_LRU = DEPTH // 2
EPS = 1e-6
MLA_HEADS = 32
Q_LORA = 1024
KV_LORA = 512
NOPE_DIM = 128
ROPE_DIM = 64
V_DIM = 128
QK_DIM = NOPE_DIM + ROPE_DIM
QK_GAIN = NOPE_DIM + ROPE_DIM // 2
ROPE_THETA = 10000.0
Q_BLOCK = 128
ATTN_SCALE = QK_DIM ** -0.5
LRU_WIDTH = D_MODEL
LRU_BLOCKS = 16
LRU_BLOCK = LRU_WIDTH // LRU_BLOCKS
CONV_W = 4
RG_C = 8.0
N_KEYS = 128
N_EXPERTS = N_KEYS * N_KEYS
PEER_HEADS = 8
PEER_QDIM = 256
PEER_HALF = PEER_QDIM // 2
PEER_TOPK = 16
PEER_BLOCK = 128

kernel_name = 'hybrid_mla_rglru_peer_stream_step'


def rms_norm(x, g):
    xf = x.astype(jnp.float32)
    y = xf * lax.rsqrt(jnp.mean(xf * xf, axis=-1, keepdims=True) + EPS)
    return (y * g.astype(jnp.float32)).astype(x.dtype)


def modulate(x, g, shift, scale):
    return rms_norm(x, g) * (1 + scale[:, None, :]) + shift[:, None, :]


def rope(x, pos):
    half = ROPE_DIM // 2
    inv = jnp.power(ROPE_THETA, -jnp.arange(half, dtype=jnp.float32) / half)
    ang = pos.astype(jnp.float32)[:, None] * inv[None, :]
    shape = ang.shape[:1] + (1,) * (x.ndim - 3) + ang.shape[1:]
    cos = jnp.cos(ang).reshape(shape).astype(x.dtype)
    sin = jnp.sin(ang).reshape(shape).astype(x.dtype)
    x1, x2 = x[..., :half], x[..., half:]
    return jnp.concatenate([x1 * cos - x2 * sin, x2 * cos + x1 * sin], axis=-1)


def head_norm(x, g):
    gain = jnp.concatenate([g[:NOPE_DIM], g[NOPE_DIM:], g[NOPE_DIM:]])
    return rms_norm(x, gain)


def attend(q, k, v, q_pos, k_pos):
    k_chunk = k_pos // CHUNK

    def block(args):
        qb, qp = args
        s = jnp.einsum('bqhd,bkhd->bhqk', qb, k).astype(jnp.float32) * ATTN_SCALE
        mask = k_chunk[None, :] <= (qp // CHUNK)[:, None]
        s = jnp.where(mask[None, None], s, -jnp.inf)
        p = jax.nn.softmax(s, axis=-1).astype(v.dtype)
        return jnp.einsum('bhqk,bkhd->bqhd', p, v)

    B, Sq, H, Dq = q.shape
    if Sq > Q_BLOCK and Sq % Q_BLOCK == 0:
        nb = Sq // Q_BLOCK
        qb = q.reshape(B, nb, Q_BLOCK, H, Dq).swapaxes(0, 1)
        out = lax.map(block, (qb, q_pos.reshape(nb, Q_BLOCK)))
        return out.swapaxes(0, 1).reshape(B, Sq, H, V_DIM)
    return block((q, q_pos))


def mla_mixer(h, pos, w_in, q_lat_g, kv_lat_g, w_uq, w_ukv, q_g, k_g, w_o, past):
    B, S, _ = h.shape
    z = h @ w_in
    cq, ckv, kpe = jnp.split(z, [Q_LORA, Q_LORA + KV_LORA], axis=-1)
    cq = rms_norm(cq, q_lat_g)
    ckv = rms_norm(ckv, kv_lat_g)
    kpe = rope(kpe, pos)
    q = head_norm((cq @ w_uq).reshape(B, S, MLA_HEADS, QK_DIM), q_g)
    q = jnp.concatenate([q[..., :NOPE_DIM], rope(q[..., NOPE_DIM:], pos)], axis=-1)
    if past is None:
        ckv_all, kpe_all, k_pos = ckv, kpe, pos
    else:
        p_ckv, p_kpe = past
        P = p_ckv.shape[1]
        ckv_all = jnp.concatenate([p_ckv.astype(ckv.dtype), ckv], axis=1)
        kpe_all = jnp.concatenate([p_kpe.astype(kpe.dtype), kpe], axis=1)
        k_pos = jnp.concatenate([jnp.arange(P, dtype=jnp.int32), pos])
    Sk = ckv_all.shape[1]
    kv = (ckv_all @ w_ukv).reshape(B, Sk, MLA_HEADS, NOPE_DIM + V_DIM)
    k_nope, v = kv[..., :NOPE_DIM], kv[..., NOPE_DIM:]
    k_pe = jnp.broadcast_to(kpe_all[:, :, None, :], (B, Sk, MLA_HEADS, ROPE_DIM))
    k = head_norm(jnp.concatenate([k_nope, k_pe], axis=-1), k_g)
    o = attend(q, k, v, pos, k_pos)
    return o.reshape(B, S, MLA_HEADS * V_DIM) @ w_o, ckv, kpe


def _lin_combine(l, r):
    a1, b1 = l
    a2, b2 = r
    return a1 * a2, a2 * b1 + b2


def rglru_mixer(h, w_in, conv_w, conv_b, w_a, b_a, w_x, b_x, lam, w_out, conv_hist, h0):
    B, S, _ = h.shape
    z = h @ w_in
    gate_br, xr = z[..., :LRU_WIDTH], z[..., LRU_WIDTH:]
    xp = jnp.concatenate([conv_hist.astype(xr.dtype), xr], axis=1)
    xc = conv_b + xp[:, 0:S] * conv_w[0]
    for j in range(1, CONV_W):
        xc = xc + xp[:, j:j + S] * conv_w[j]
    new_conv = xp[:, -(CONV_W - 1):]
    xb = xc.reshape(B, S, LRU_BLOCKS, LRU_BLOCK)
    r = jax.nn.sigmoid((jnp.einsum('bsnd,nde->bsne', xb, w_a) + b_a).astype(jnp.float32)).reshape(B, S, LRU_WIDTH)
    i = jax.nn.sigmoid((jnp.einsum('bsnd,nde->bsne', xb, w_x) + b_x).astype(jnp.float32)).reshape(B, S, LRU_WIDTH)
    log_a = -RG_C * r * jax.nn.softplus(-lam.astype(jnp.float32))
    a = jnp.exp(log_a)
    u = jnp.sqrt(-jnp.expm1(2 * log_a)) * (i * xc.astype(jnp.float32))
    u = u.at[:, 0].add(a[:, 0] * h0.astype(jnp.float32))
    _, hs = lax.associative_scan(_lin_combine, (a, u), axis=1)
    y = hs.astype(h.dtype) * jax.nn.gelu(gate_br)
    return y @ w_out, new_conv, hs[:, -1].astype(h.dtype)


def peer_block(tb, w_q, sub_keys, u, v):
    TB = tb.shape[0]
    q = (tb @ w_q).astype(jnp.float32).reshape(TB, PEER_HEADS, 2, PEER_HALF)
    s = jnp.einsum('thpd,pnd->thpn', q, sub_keys.astype(jnp.float32))
    sv, si = lax.top_k(s, PEER_TOPK)
    cand_s = (sv[:, :, 0, :, None] + sv[:, :, 1, None, :]).reshape(TB, PEER_HEADS, PEER_TOPK * PEER_TOPK)
    cand_i = (si[:, :, 0, :, None] * N_KEYS + si[:, :, 1, None, :]).reshape(TB, PEER_HEADS, PEER_TOPK * PEER_TOPK)
    best, pick = lax.top_k(cand_s, PEER_TOPK)
    eid = jnp.take_along_axis(cand_i, pick, axis=-1)
    g = jax.nn.softmax(best, axis=-1)
    ue = u[eid]
    ve = v[eid]
    act = jax.nn.gelu(jnp.einsum('thkd,td->thk', ue, tb).astype(jnp.float32))
    return jnp.einsum('thk,thkd->td', (g * act).astype(tb.dtype), ve)


def peer(h, w_q, sub_keys, u, v):
    B, S, D = h.shape
    T = B * S
    nb = -(-T // PEER_BLOCK)
    t = jnp.pad(h.reshape(T, D), ((0, nb * PEER_BLOCK - T), (0, 0))).reshape(nb, PEER_BLOCK, D)
    out = lax.map(lambda tb: peer_block(tb, w_q, sub_keys, u, v), t)
    return out.reshape(nb * PEER_BLOCK, D)[:T].reshape(B, S, D)


def forward(x, c, pos, W, past):
    B = x.shape[0]
    ckvs, kpes, convs, hss = [], [], [], []
    cs = jax.nn.silu(c)
    for i in range(DEPTH):
        mod = cs @ W['w_ada'][i] + W['b_ada'][i]
        sh_m, sc_m, g_m, sh_f, sc_f, g_f = jnp.split(mod, 6, axis=-1)
        hm = modulate(x, W['norm_mix'][i], sh_m, sc_m)
        j = i // N_MIXERS
        if i % N_MIXERS == 0:
            p = None if past is None else (past['ckv'][j], past['kpe'][j])
            out, ckv, kpe = mla_mixer(hm, pos, W['mla_w_in'][j], W['mla_q_lat_norm'][j], W['mla_kv_lat_norm'][j],
                                      W['mla_w_uq'][j], W['mla_w_ukv'][j], W['mla_q_norm'][j], W['mla_k_norm'][j],
                                      W['mla_w_o'][j], p)
            ckvs.append(ckv)
            kpes.append(kpe)
        else:
            if past is None:
                conv_hist = jnp.zeros((B, CONV_W - 1, LRU_WIDTH), x.dtype)
                h0 = jnp.zeros((B, LRU_WIDTH), x.dtype)
            else:
                conv_hist = past['conv'][j]
                h0 = past['h'][j]
            out, conv_new, h_new = rglru_mixer(hm, W['lru_w_in'][j], W['lru_conv_w'][j], W['lru_conv_b'][j],
                                               W['lru_w_a'][j], W['lru_b_a'][j], W['lru_w_x'][j], W['lru_b_x'][j],
                                               W['lru_lambda'][j], W['lru_w_out'][j], conv_hist, h0)
            convs.append(conv_new)
            hss.append(h_new)
        x = x + g_m[:, None, :] * out
        hf = modulate(x, W['norm_ffn'][i], sh_f, sc_f)
        x = x + g_f[:, None, :] * peer(hf, W['peer_w_q'][i], W['peer_sub_keys'][i], W['peer_u'][i], W['peer_v'][i])
    return x, jnp.stack(ckvs), jnp.stack(kpes), jnp.stack(convs), jnp.stack(hss)


def setup_inputs(seed: int = 0) -> dict:
    key = jax.random.key(seed)
    ks = iter(jax.random.split(key, 40))
    f32 = jnp.float32

    def nrm(shape, scale):
        return jax.random.normal(next(ks), shape, f32) * scale

    def gain(shape):
        return 1.0 + 0.02 * jax.random.normal(next(ks), shape, f32)

    a0 = jax.random.uniform(next(ks), (N_LRU, LRU_WIDTH), f32, minval=0.9, maxval=0.999)
    lam = jnp.log(a0) - jnp.log1p(-a0)
    return {
        'x_prompt': nrm((BATCH, SEQ, D_MODEL), 1.0),
        'x_sample': nrm((DEC_BATCH, DEC_SEQ, D_MODEL), 1.0),
        'c_prompt': nrm((BATCH, D_MODEL), 1.0),
        'c_sample': nrm((DEC_BATCH, D_MODEL), 1.0),
        'cache_ckv': nrm((N_MLA, DEC_BATCH, PAST_LEN, KV_LORA), 1.0),
        'cache_kpe': nrm((N_MLA, DEC_BATCH, PAST_LEN, ROPE_DIM), 1.0),
        'state_conv': nrm((N_LRU, DEC_BATCH, CONV_W - 1, LRU_WIDTH), 1.0),
        'state_h': nrm((N_LRU, DEC_BATCH, LRU_WIDTH), 0.5),
        'norm_mix': gain((DEPTH, D_MODEL)),
        'norm_ffn': gain((DEPTH, D_MODEL)),
        'w_ada': nrm((DEPTH, D_MODEL, 6 * D_MODEL), 0.5 * D_MODEL ** -0.5),
        'b_ada': nrm((DEPTH, 6 * D_MODEL), 0.02),
        'mla_w_in': nrm((N_MLA, D_MODEL, Q_LORA + KV_LORA + ROPE_DIM), D_MODEL ** -0.5),
        'mla_q_lat_norm': gain((N_MLA, Q_LORA)),
        'mla_kv_lat_norm': gain((N_MLA, KV_LORA)),
        'mla_w_uq': nrm((N_MLA, Q_LORA, MLA_HEADS * QK_DIM), Q_LORA ** -0.5),
        'mla_w_ukv': nrm((N_MLA, KV_LORA, MLA_HEADS * (NOPE_DIM + V_DIM)), KV_LORA ** -0.5),
        'mla_q_norm': gain((N_MLA, QK_GAIN)),
        'mla_k_norm': gain((N_MLA, QK_GAIN)),
        'mla_w_o': nrm((N_MLA, MLA_HEADS * V_DIM, D_MODEL), (MLA_HEADS * V_DIM) ** -0.5),
        'lru_w_in': nrm((N_LRU, D_MODEL, 2 * LRU_WIDTH), D_MODEL ** -0.5),
        'lru_conv_w': nrm((N_LRU, CONV_W, LRU_WIDTH), CONV_W ** -0.5),
        'lru_conv_b': nrm((N_LRU, LRU_WIDTH), 0.02),
        'lru_w_a': nrm((N_LRU, LRU_BLOCKS, LRU_BLOCK, LRU_BLOCK), LRU_BLOCK ** -0.5),
        'lru_b_a': nrm((N_LRU, LRU_BLOCKS, LRU_BLOCK), 0.02),
        'lru_w_x': nrm((N_LRU, LRU_BLOCKS, LRU_BLOCK, LRU_BLOCK), LRU_BLOCK ** -0.5),
        'lru_b_x': nrm((N_LRU, LRU_BLOCKS, LRU_BLOCK), 0.02),
        'lru_lambda': lam,
        'lru_w_out': nrm((N_LRU, LRU_WIDTH, D_MODEL), LRU_WIDTH ** -0.5),
        'peer_w_q': nrm((DEPTH, D_MODEL, PEER_HEADS * PEER_QDIM), D_MODEL ** -0.5),
        'peer_sub_keys': nrm((DEPTH, 2, N_KEYS, PEER_HALF), PEER_HALF ** -0.5),
        'peer_u': nrm((DEPTH, N_EXPERTS, D_MODEL), D_MODEL ** -0.5),
        'peer_v': nrm((DEPTH, N_EXPERTS, D_MODEL), PEER_HEADS ** -0.5),
    }


def reference(x_prompt, x_sample, c_prompt, c_sample, cache_ckv, cache_kpe, state_conv, state_h,
              norm_mix, norm_ffn, w_ada, b_ada,
              mla_w_in, mla_q_lat_norm, mla_kv_lat_norm, mla_w_uq, mla_w_ukv, mla_q_norm, mla_k_norm, mla_w_o,
              lru_w_in, lru_conv_w, lru_conv_b, lru_w_a, lru_b_a, lru_w_x, lru_b_x, lru_lambda, lru_w_out,
              peer_w_q, peer_sub_keys, peer_u, peer_v):
    W = dict(norm_mix=norm_mix, norm_ffn=norm_ffn, w_ada=w_ada, b_ada=b_ada,
             mla_w_in=mla_w_in, mla_q_lat_norm=mla_q_lat_norm, mla_kv_lat_norm=mla_kv_lat_norm,
             mla_w_uq=mla_w_uq, mla_w_ukv=mla_w_ukv, mla_q_norm=mla_q_norm, mla_k_norm=mla_k_norm, mla_w_o=mla_w_o,
             lru_w_in=lru_w_in, lru_conv_w=lru_conv_w, lru_conv_b=lru_conv_b, lru_w_a=lru_w_a, lru_b_a=lru_b_a,
             lru_w_x=lru_w_x, lru_b_x=lru_b_x, lru_lambda=lru_lambda, lru_w_out=lru_w_out,
             peer_w_q=peer_w_q, peer_sub_keys=peer_sub_keys, peer_u=peer_u, peer_v=peer_v)
    pos_p = jnp.arange(x_prompt.shape[1], dtype=jnp.int32)
    pos_s = cache_ckv.shape[2] + jnp.arange(x_sample.shape[1], dtype=jnp.int32)
    y_prompt, ckv_p, kpe_p, conv_p, h_p = forward(x_prompt, c_prompt, pos_p, W, None)
    past = dict(ckv=cache_ckv, kpe=cache_kpe, conv=state_conv, h=state_h)
    y_sample, ckv_s, kpe_s, conv_s, h_s = forward(x_sample, c_sample, pos_s, W, past)
    return (y_prompt, y_sample, ckv_p, kpe_p, ckv_s, kpe_s, conv_p, h_p, conv_s, h_s)
```

```python
import functools

import jax
import jax.numpy as jnp
from jax import lax
from jax.experimental import pallas as pl
from jax.experimental.pallas import tpu as pltpu

DEPTH = 4
CHUNK = 64
EPS = 1e-6
MLA_HEADS = 32
Q_LORA = 1024
KV_LORA = 512
NOPE_DIM = 128
ROPE_DIM = 64
V_DIM = 128
QK_DIM = NOPE_DIM + ROPE_DIM
ROPE_THETA = 10000.0
ATTN_SCALE = QK_DIM ** -0.5
LRU_BLOCKS = 16
CONV_W = 4
RG_C = 8.0
N_KEYS = 128
PEER_HEADS = 8
PEER_QDIM = 256
PEER_HALF = PEER_QDIM // 2
PEER_TOPK = 16

LANES = 128
SUBLANES = 8
VMEM_LIMIT = 56 * 1024 * 1024

F32 = jnp.float32
BF16 = jnp.bfloat16


def _tile(n, target, mult):
    best = None
    d = mult
    while d <= min(n, target):
        if n % d == 0:
            best = d
        d += mult
    return n if best is None else best


def _params(*sem):
    return pltpu.CompilerParams(dimension_semantics=sem, vmem_limit_bytes=VMEM_LIMIT)


def _gelu(x):
    return 0.5 * x * (1.0 + jnp.tanh(0.7978845608028654 * (x + 0.044715 * x * x * x)))


def _ada_kernel(c_ref, w_ref, b_ref, o_ref):
    c = c_ref[...]
    cs = (c * jax.nn.sigmoid(c)).astype(BF16)
    o_ref[0] = jnp.dot(cs, w_ref[0].astype(BF16), preferred_element_type=F32) + b_ref[0]


def _ada(c, w_ada, b_ada):
    L, D, N = w_ada.shape
    nb = c.shape[0]
    tn = _tile(N, 512, LANES)
    return pl.pallas_call(
        _ada_kernel,
        grid=(L, N // tn),
        in_specs=[pl.BlockSpec((nb, D), lambda l, j: (0, 0)),
                  pl.BlockSpec((1, D, tn), lambda l, j: (l, 0, j)),
                  pl.BlockSpec((1, 1, tn), lambda l, j: (l, 0, j))],
        out_specs=pl.BlockSpec((1, nb, tn), lambda l, j: (l, 0, j)),
        out_shape=jax.ShapeDtypeStruct((L, nb, N), F32),
        compiler_params=_params("parallel", "parallel"),
        name="ada",
    )(c, w_ada, b_ada.reshape(L, 1, N))


def _norm_mod_kernel(x_ref, g_ref, sh_ref, sc_ref, o_ref):
    x = x_ref[0]
    y = x * lax.rsqrt(jnp.mean(x * x, axis=-1, keepdims=True) + EPS)
    y = y * g_ref[...]
    o_ref[0] = (y * (1.0 + sc_ref[0]) + sh_ref[0]).astype(o_ref.dtype)


def _norm_mod(x, g, shift, scale):
    B, S, D = x.shape
    ts = _tile(S, 256, 16)
    vec = pl.BlockSpec((1, 1, D), lambda b, s: (b, 0, 0))
    return pl.pallas_call(
        _norm_mod_kernel,
        grid=(B, S // ts),
        in_specs=[pl.BlockSpec((1, ts, D), lambda b, s: (b, s, 0)),
                  pl.BlockSpec((1, D), lambda b, s: (0, 0)), vec, vec],
        out_specs=pl.BlockSpec((1, ts, D), lambda b, s: (b, s, 0)),
        out_shape=jax.ShapeDtypeStruct((B, S, D), BF16),
        compiler_params=_params("parallel", "parallel"),
        name="norm_mod",
    )(x, g.reshape(1, D), shift, scale)


def _mm_kernel(*refs, nk, has_res, nb):
    if has_res:
        a_ref, b_ref, res_ref, gate_ref, o_ref = refs[:5]
        scratch = refs[5:]
    else:
        a_ref, b_ref, o_ref = refs[:3]
        scratch = refs[3:]

    def finish(acc):
        if has_res:
            tm, tn = acc.shape
            acc = (acc.reshape(nb, tm // nb, tn) * gate_ref[...]).reshape(tm, tn)
            acc = res_ref[...] + acc
        o_ref[...] = acc.astype(o_ref.dtype)

    part = jnp.dot(a_ref[...].astype(BF16), b_ref[...].astype(BF16), preferred_element_type=F32)
    if nk == 1:
        finish(part)
    else:
        acc_ref = scratch[0]
        k = pl.program_id(2)

        @pl.when(k == 0)
        def _():
            acc_ref[...] = part

        @pl.when(k > 0)
        def _():
            acc_ref[...] += part

        @pl.when(k == nk - 1)
        def _():
            finish(acc_ref[...])


def _matmul(a, b, out_dtype, *, res=None, gate=None, seq=None, tm=1024, tn=512, tk=4096, name="mm"):
    M, K = a.shape
    N = b.shape[1]
    has_res = res is not None
    if has_res:
        tm = _tile(M, tm, seq) if seq <= tm else _tile(seq, tm, 16)
        nb = max(tm // seq, 1)
    else:
        tm = _tile(M, tm, 16)
        nb = 1
    tn = _tile(N, tn, LANES)
    tk = _tile(K, tk, LANES)
    nk = K // tk
    in_specs = [pl.BlockSpec((tm, tk), lambda i, j, k: (i, k)),
                pl.BlockSpec((tk, tn), lambda i, j, k: (k, j))]
    args = [a, b]
    if has_res:
        in_specs.append(pl.BlockSpec((tm, tn), lambda i, j, k: (i, j)))
        if tm >= seq:
            in_specs.append(pl.BlockSpec((nb, 1, tn), lambda i, j, k: (i, 0, j)))
        else:
            per = seq // tm
            in_specs.append(pl.BlockSpec((1, 1, tn), lambda i, j, k: (i // per, 0, j)))
        args += [res, gate]
    return pl.pallas_call(
        functools.partial(_mm_kernel, nk=nk, has_res=has_res, nb=nb),
        grid=(M // tm, N // tn, nk),
        in_specs=in_specs,
        out_specs=pl.BlockSpec((tm, tn), lambda i, j, k: (i, j)),
        out_shape=jax.ShapeDtypeStruct((M, N), out_dtype),
        scratch_shapes=[] if nk == 1 else [pltpu.VMEM((tm, tn), F32)],
        compiler_params=_params("parallel", "parallel", "arbitrary"),
        name=name,
    )(*args)


def _rope_tables(pos):
    half = ROPE_DIM // 2
    inv = jnp.power(ROPE_THETA, -jnp.arange(half, dtype=F32) / half)
    ang = pos.astype(F32)[:, None] * inv[None, :]
    cos, sin = jnp.cos(ang), jnp.sin(ang)
    pad = jnp.zeros((pos.shape[0], LANES - ROPE_DIM), F32)
    return (jnp.concatenate([cos, cos, pad], axis=1), jnp.concatenate([sin, sin, pad], axis=1))


def _rot_cols(w):
    half = ROPE_DIM // 2
    return jnp.concatenate([-w[..., half:], w[..., :half]], axis=-1)


def _mla_post_kernel(z_ref, gq_ref, gkv_ref, cos_ref, sin_ref, cq_ref, ckv_ref, kpe_ref):
    z = z_ref[...]
    cq = z[:, :Q_LORA]
    cq = cq * lax.rsqrt(jnp.mean(cq * cq, axis=-1, keepdims=True) + EPS) * gq_ref[...]
    cq_ref[...] = cq.astype(cq_ref.dtype)
    ckv = z[:, Q_LORA:Q_LORA + KV_LORA]
    ckv = ckv * lax.rsqrt(jnp.mean(ckv * ckv, axis=-1, keepdims=True) + EPS) * gkv_ref[...]
    ckv_ref[...] = ckv
    o = Q_LORA + KV_LORA
    kpe = z[:, o:o + LANES] * cos_ref[...] + z[:, o + LANES:o + 2 * LANES] * sin_ref[...]
    kpe_ref[...] = kpe[:, :ROPE_DIM]


def _mla_post(z, gq, gkv, cos, sin, S):
    T, ZW = z.shape
    tm = _tile(S, 512, 16)
    per = S // tm
    row = lambda i: (i, 0)
    tab = pl.BlockSpec((tm, LANES), lambda i: (i % per, 0))
    return pl.pallas_call(
        _mla_post_kernel,
        grid=(T // tm,),
        in_specs=[pl.BlockSpec((tm, ZW), row),
                  pl.BlockSpec((1, Q_LORA), lambda i: (0, 0)),
                  pl.BlockSpec((1, KV_LORA), lambda i: (0, 0)), tab, tab],
        out_specs=[pl.BlockSpec((tm, Q_LORA), row), pl.BlockSpec((tm, KV_LORA), row),
                   pl.BlockSpec((tm, ROPE_DIM), row)],
        out_shape=[jax.ShapeDtypeStruct((T, Q_LORA), BF16), jax.ShapeDtypeStruct((T, KV_LORA), F32),
                   jax.ShapeDtypeStruct((T, ROPE_DIM), F32)],
        compiler_params=_params("parallel"),
        name="mla_post",
    )(z, gq.reshape(1, -1), gkv.reshape(1, -1), cos, sin)


def _q_heads_kernel(a_ref, w_ref, cos_ref, sin_ref, gn_ref, gr_ref, o_ref, *, hb):
    a = a_ref[...]
    lane = lax.broadcasted_iota(jnp.int32, (1, LANES), 1)
    for h in range(hb):
        r = jnp.dot(a, w_ref[h], preferred_element_type=F32)
        nope = r[:, :NOPE_DIM]
        rr = r[:, NOPE_DIM:]
        ss = jnp.sum(nope * nope, axis=-1, keepdims=True) + jnp.sum(
            jnp.where(lane < ROPE_DIM, rr * rr, 0.0), axis=-1, keepdims=True)
        sc = lax.rsqrt(ss * (1.0 / QK_DIM) + EPS) * ATTN_SCALE
        roped = rr * cos_ref[...] + pltpu.roll(rr, ROPE_DIM, axis=1) * sin_ref[...]
        o_ref[0, h, :, :NOPE_DIM] = (nope * sc * gn_ref[...]).astype(o_ref.dtype)
        o_ref[0, h, :, NOPE_DIM:] = (roped * sc * gr_ref[...])[:, :ROPE_DIM].astype(o_ref.dtype)


def _q_heads(cq, w, cos, sin, gn, gr, B, S):
    T = B * S
    H = w.shape[0]
    hb = _tile(H, 8, 1)
    tm = _tile(S, 512, 16)
    per = S // tm
    tab = pl.BlockSpec((tm, LANES), lambda i, h: (i % per, 0))
    vec = pl.BlockSpec((1, LANES), lambda i, h: (0, 0))
    return pl.pallas_call(
        functools.partial(_q_heads_kernel, hb=hb),
        grid=(T // tm, H // hb),
        in_specs=[pl.BlockSpec((tm, Q_LORA), lambda i, h: (i, 0)),
                  pl.BlockSpec((hb, Q_LORA, 2 * LANES), lambda i, h: (h, 0, 0)),
                  tab, tab, vec, vec],
        out_specs=pl.BlockSpec((1, hb, tm, QK_DIM), lambda i, h: (i // per, h, i % per, 0)),
        out_shape=jax.ShapeDtypeStruct((B, H, S, QK_DIM), BF16),
        compiler_params=_params("parallel", "parallel"),
        name="q_heads",
    )(cq, w, cos, sin, gn, gr)


def _kv_heads_kernel(a_ref, kpe_ref, w_ref, gn_ref, gr_ref, k_ref, v_ref, *, hb):
    a = a_ref[...].astype(BF16)
    kp = kpe_ref[...]
    kp_ss = jnp.sum(kp * kp, axis=-1, keepdims=True)
    for h in range(hb):
        r = jnp.dot(a, w_ref[h], preferred_element_type=F32)
        kn = r[:, :NOPE_DIM]
        ss = jnp.sum(kn * kn, axis=-1, keepdims=True) + kp_ss
        sc = lax.rsqrt(ss * (1.0 / QK_DIM) + EPS)
        k_ref[0, h, :, :NOPE_DIM] = (kn * sc * gn_ref[...]).astype(k_ref.dtype)
        k_ref[0, h, :, NOPE_DIM:] = (kp * sc * gr_ref[...]).astype(k_ref.dtype)
        v_ref[0, h] = r[:, NOPE_DIM:].astype(v_ref.dtype)


def _kv_heads(ckv, kpe, w, gn, gr, B, Sk):
    T = B * Sk
    H = w.shape[0]
    hb = _tile(H, 8, 1)
    tm = _tile(Sk, 512, 16)
    per = Sk // tm
    out_idx = lambda i, h: (i // per, h, i % per, 0)
    return pl.pallas_call(
        functools.partial(_kv_heads_kernel, hb=hb),
        grid=(T // tm, H // hb),
        in_specs=[pl.BlockSpec((tm, KV_LORA), lambda i, h: (i, 0)),
                  pl.BlockSpec((tm, ROPE_DIM), lambda i, h: (i, 0)),
                  pl.BlockSpec((hb, KV_LORA, NOPE_DIM + V_DIM), lambda i, h: (h, 0, 0)),
                  pl.BlockSpec((1, NOPE_DIM), lambda i, h: (0, 0)),
                  pl.BlockSpec((1, ROPE_DIM), lambda i, h: (0, 0))],
        out_specs=[pl.BlockSpec((1, hb, tm, QK_DIM), out_idx), pl.BlockSpec((1, hb, tm, V_DIM), out_idx)],
        out_shape=[jax.ShapeDtypeStruct((B, H, Sk, QK_DIM), BF16), jax.ShapeDtypeStruct((B, H, Sk, V_DIM), BF16)],
        compiler_params=_params("parallel", "parallel"),
        name="kv_heads",
    )(ckv, kpe, w, gn, gr)


def _flash_kernel(q_ref, k_ref, v_ref, o_ref, *, tq, tk, nk, q_off):
    shift = CHUNK.bit_length() - 1
    q = q_ref[0, 0]
    q0 = q_off + pl.program_id(2) * tq
    first_c = lax.shift_right_logical(q0, shift)
    last_c = lax.shift_right_logical(q0 + tq - 1, shift)
    n_vis = jnp.minimum(lax.div((first_c + 1) * CHUNK, tk), nk)
    n_need = jnp.minimum(lax.div((last_c + 1) * CHUNK + tk - 1, tk), nk)

    def step(ki, carry, masked):
        m, l, acc = carry
        start = pl.multiple_of(ki * tk, tk)
        kb = k_ref[0, 0, pl.ds(start, tk), :]
        vb = v_ref[0, 0, pl.ds(start, tk), :]
        s = lax.dot_general(q, kb, (((1,), (1,)), ((), ())), preferred_element_type=F32)
        if masked:
            qc = lax.shift_right_logical(q0 + lax.broadcasted_iota(jnp.int32, (tq, tk), 0), shift)
            kc = lax.shift_right_logical(start + lax.broadcasted_iota(jnp.int32, (tq, tk), 1), shift)
            s = jnp.where(kc <= qc, s, -jnp.inf)
        m_new = jnp.maximum(m, jnp.max(s, axis=-1, keepdims=True))
        alpha = jnp.exp(m - m_new)
        p = jnp.exp(s - m_new)
        l = alpha * l + jnp.sum(p, axis=-1, keepdims=True)
        acc = alpha * acc + jnp.dot(p.astype(vb.dtype), vb, preferred_element_type=F32)
        return m_new, l, acc

    carry = (jnp.full((tq, 1), -jnp.inf, F32), jnp.zeros((tq, 1), F32), jnp.zeros((tq, V_DIM), F32))
    carry = lax.fori_loop(0, n_vis, functools.partial(step, masked=False), carry)
    carry = lax.fori_loop(n_vis, n_need, functools.partial(step, masked=True), carry)
    _, l, acc = carry
    o_ref[0] = (acc / l).astype(o_ref.dtype)


def _flash(q, k, v, q_off):
    B, H, Sq, _ = q.shape
    Sk = k.shape[2]
    tq = _tile(Sq, 512, 16)
    tk = _tile(Sk, 512, 16)
    return pl.pallas_call(
        functools.partial(_flash_kernel, tq=tq, tk=tk, nk=Sk // tk, q_off=q_off),
        grid=(B, H, Sq // tq),
        in_specs=[pl.BlockSpec((1, 1, tq, QK_DIM), lambda b, h, i: (b, h, i, 0)),
                  pl.BlockSpec((1, 1, Sk, QK_DIM), lambda b, h, i: (b, h, 0, 0)),
                  pl.BlockSpec((1, 1, Sk, V_DIM), lambda b, h, i: (b, h, 0, 0))],
        out_specs=pl.BlockSpec((1, tq, V_DIM), lambda b, h, i: (b, i, h)),
        out_shape=jax.ShapeDtypeStruct((B, Sq, H * V_DIM), BF16),
        compiler_params=_params("parallel", "parallel", "parallel"),
        name="flash",
    )(q, k, v)


def _mla_layer(x, hm, gate, pos_off, w, past):
    B, S, D = x.shape
    T = B * S
    cos, sin = _rope_tables(pos_off + jnp.arange(S, dtype=jnp.int32))
    z = _matmul(hm.reshape(T, D), w["w_in"], F32, name="mla_in")
    cq, ckv, kpe = _mla_post(z, w["q_lat_g"], w["kv_lat_g"], cos, sin, S)
    q = _q_heads(cq, w["w_uq"], cos, sin, w["qg_nope"], w["qg_rope"], B, S)
    ckv3 = ckv.reshape(B, S, KV_LORA)
    kpe3 = kpe.reshape(B, S, ROPE_DIM)
    if past is None:
        ckv_all, kpe_all = ckv3, kpe3
    else:
        ckv_all = jnp.concatenate([past[0], ckv3], axis=1)
        kpe_all = jnp.concatenate([past[1], kpe3], axis=1)
    Sk = ckv_all.shape[1]
    k, v = _kv_heads(ckv_all.reshape(B * Sk, KV_LORA), kpe_all.reshape(B * Sk, ROPE_DIM),
                     w["w_ukv"], w["kg_nope"], w["kg_rope"], B, Sk)
    o = _flash(q, k, v, pos_off)
    x_new = _matmul(o.reshape(T, -1), w["w_o"], F32, res=x.reshape(T, D), gate=gate, seq=S, name="mla_out")
    return x_new.reshape(B, S, D), ckv3, kpe3


def _lru_kernel(gate_ref, xr_ref, hist_ref, h0_ref, cw_ref, cb_ref, wa_ref, ba_ref, wx_ref, bx_ref, lam_ref,
                y_ref, hl_ref, xbuf, a_buf, u_buf, h_buf, *, tt, nt):
    t = pl.program_id(2)
    nhist = CONV_W - 1

    @pl.when(t == 0)
    def _():
        xbuf[SUBLANES - nhist:SUBLANES, :] = hist_ref[0]
        h_buf[...] = h0_ref[0]

    xbuf[SUBLANES:SUBLANES + tt, :] = xr_ref[0]
    xc = cb_ref[...]
    for j in range(CONV_W):
        o = SUBLANES - nhist + j
        xc = xc + xbuf[o:o + tt, :] * cw_ref[j:j + 1, :]
    xbuf[0:SUBLANES, :] = xbuf[tt:tt + SUBLANES, :]

    xcb = xc.astype(BF16)
    r = jax.nn.sigmoid(jnp.dot(xcb, wa_ref[0], preferred_element_type=F32) + ba_ref[0])
    i = jax.nn.sigmoid(jnp.dot(xcb, wx_ref[0], preferred_element_type=F32) + bx_ref[0])
    nl = -lam_ref[...]
    softplus = jnp.maximum(nl, 0.0) + jnp.log1p(jnp.exp(-jnp.abs(nl)))
    log_a = -RG_C * r * softplus
    a = jnp.exp(log_a)
    u = jnp.sqrt(-jnp.tanh(log_a) * (a * a + 1.0)) * (i * xc)
    a_buf[...] = a
    u_buf[...] = u

    row = lax.broadcasted_iota(jnp.int32, (SUBLANES, 1), 0)

    def group(g, h):
        start = pl.multiple_of(g * SUBLANES, SUBLANES)
        a8 = a_buf[pl.ds(start, SUBLANES), :]
        u8 = u_buf[pl.ds(start, SUBLANES), :]
        d = 1
        while d < SUBLANES:
            ok = row >= d
            u8 = jnp.where(ok, a8 * pltpu.roll(u8, d, axis=0) + u8, u8)
            a8 = jnp.where(ok, a8 * pltpu.roll(a8, d, axis=0), a8)
            d *= 2
        h8 = a8 * h + u8
        u_buf[pl.ds(start, SUBLANES), :] = h8
        return h8[SUBLANES - 1:SUBLANES, :]

    h = lax.fori_loop(0, tt // SUBLANES, group, h_buf[...])
    h_buf[...] = h
    y_ref[0] = (u_buf[...] * _gelu(gate_ref[0])).astype(y_ref.dtype)

    @pl.when(t == nt - 1)
    def _():
        hl_ref[0] = h


def _lru_core(z, hist, h0, w):
    B, S, W2 = z.shape
    W = W2 // 2
    nblk = w["w_a"].shape[0]
    bw = W // nblk
    tt = _tile(S, 512, 16)
    nt = S // tt
    col = lambda b, n, t: (0, n)
    blk = lambda b, n, t: (n, 0, 0)
    y, hl = pl.pallas_call(
        functools.partial(_lru_kernel, tt=tt, nt=nt),
        grid=(B, nblk, nt),
        in_specs=[pl.BlockSpec((1, tt, bw), lambda b, n, t: (b, t, n)),
                  pl.BlockSpec((1, tt, bw), lambda b, n, t: (b, t, nblk + n)),
                  pl.BlockSpec((1, CONV_W - 1, bw), lambda b, n, t: (b, 0, n)),
                  pl.BlockSpec((1, 1, bw), lambda b, n, t: (b, 0, n)),
                  pl.BlockSpec((CONV_W, bw), col),
                  pl.BlockSpec((1, bw), col),
                  pl.BlockSpec((1, bw, bw), blk),
                  pl.BlockSpec((1, 1, bw), blk),
                  pl.BlockSpec((1, bw, bw), blk),
                  pl.BlockSpec((1, 1, bw), blk),
                  pl.BlockSpec((1, bw), col)],
        out_specs=[pl.BlockSpec((1, tt, bw), lambda b, n, t: (b, t, n)),
                   pl.BlockSpec((1, 1, bw), lambda b, n, t: (b, 0, n))],
        out_shape=[jax.ShapeDtypeStruct((B, S, W), BF16), jax.ShapeDtypeStruct((B, 1, W), F32)],
        scratch_shapes=[pltpu.VMEM((SUBLANES + tt, bw), F32), pltpu.VMEM((tt, bw), F32),
                        pltpu.VMEM((tt, bw), F32), pltpu.VMEM((1, bw), F32)],
        compiler_params=_params("parallel", "parallel", "arbitrary"),
        name="lru_core",
    )(z, z, hist, h0.reshape(B, 1, W), w["conv_w"], w["conv_b"].reshape(1, W), w["w_a"], w["b_a"],
      w["w_x"], w["b_x"], w["lam"].reshape(1, W))
    return y, hl.reshape(B, W)


def _lru_layer(x, hm, gate, w, past):
    B, S, D = x.shape
    T = B * S
    W = w["w_in"].shape[1] // 2
    if past is None:
        hist = jnp.zeros((B, CONV_W - 1, W), F32)
        h0 = jnp.zeros((B, W), F32)
    else:
        hist, h0 = past
    z = _matmul(hm.reshape(T, D), w["w_in"], F32, name="lru_in").reshape(B, S, 2 * W)
    y, h_last = _lru_core(z, hist, h0, w)
    new_conv = jnp.concatenate([hist, z[:, -(CONV_W - 1):, W:]], axis=1)[:, -(CONV_W - 1):]
    x_new = _matmul(y.reshape(T, W), w["w_out"], F32, res=x.reshape(T, D), gate=gate, seq=S, name="lru_out")
    return x_new.reshape(B, S, D), new_conv, h_last


def _topk_rounds(work, k):
    rows = work.shape[0]
    idx = lax.broadcasted_iota(jnp.int32, work.shape, 0).astype(F32)
    out = []
    for _ in range(k):
        m = jnp.max(work, axis=0, keepdims=True)
        first = jnp.min(jnp.where(work == m, idx, float(rows)), axis=0, keepdims=True)
        work = jnp.where(idx == first, -jnp.inf, work)
        out.append((m, first))
    return out


def _route_kernel(q_ref, keys_ref, e1_ref, l1_ref, e2_ref, r2_ref):
    K = PEER_TOPK
    q = q_ref[...]
    nt = (((1,), (1,)), ((), ()))
    s1 = lax.dot_general(keys_ref[0], q[:, :PEER_HALF], nt, preferred_element_type=F32)
    s2 = lax.dot_general(keys_ref[1], q[:, PEER_HALF:], nt, preferred_element_type=F32)
    tt = s1.shape[1]
    key_idx = lax.broadcasted_iota(jnp.int32, (N_KEYS, tt), 0).astype(F32)
    krow = lax.broadcasted_iota(jnp.int32, (K, tt), 0).astype(F32)

    def ranks(s):
        rank = jnp.full((N_KEYS, tt), float(K), F32)
        vals = []
        for j, (m, first) in enumerate(_topk_rounds(s, K)):
            rank = jnp.where(key_idx == first, float(j), rank)
            vals.append(m)
        return rank, vals

    rank1, v1 = ranks(s1)
    rank2, v2 = ranks(s2)
    sv2 = jnp.zeros((K, tt), F32)
    for j in range(K):
        sv2 = jnp.where(krow == float(j), v2[j], sv2)
    cand = jnp.concatenate([v1[a] + sv2 for a in range(K)], axis=0)
    L = jnp.zeros((K, tt), F32)
    Z = jnp.zeros((1, tt), F32)
    m0 = v1[0] + v2[0]
    for m, first in _topk_rounds(cand, K):
        L = L + jnp.where(krow == jnp.floor(first * (1.0 / K)), 1.0, 0.0)
        Z = Z + jnp.exp(m - m0)
    lim1 = jnp.zeros((N_KEYS, tt), F32)
    for a in range(K):
        lim1 = jnp.where(rank1 == float(a), L[a:a + 1, :], lim1)
    e1_ref[0] = jnp.exp(s1 - v1[0])
    l1_ref[0] = lim1
    e2_ref[0] = jnp.exp(s2 - v2[0]) / Z
    r2_ref[0] = rank2


def _route(q, keys):
    T = q.shape[0]
    H = q.shape[1] // PEER_QDIM
    tt = _tile(T, 128, LANES)
    spec = pl.BlockSpec((1, N_KEYS, tt), lambda i, h: (h, 0, i))
    shape = jax.ShapeDtypeStruct((H, N_KEYS, T), F32)
    return pl.pallas_call(
        _route_kernel,
        grid=(T // tt, H),
        in_specs=[pl.BlockSpec((tt, PEER_QDIM), lambda i, h: (i, h)),
                  pl.BlockSpec((2, N_KEYS, PEER_HALF), lambda i, h: (0, 0, 0))],
        out_specs=[spec] * 4,
        out_shape=[shape] * 4,
        compiler_params=_params("parallel", "parallel"),
        name="peer_route",
    )(q, keys)


def _peer_p_kernel(x_ref, u_ref, e1_ref, l1_ref, e2_ref, r2_ref, p_ref, *, H, n1):
    act = lax.dot_general(u_ref[...], x_ref[...], (((1,), (1,)), ((), ())), preferred_element_type=F32)
    g = _gelu(act)
    tt = act.shape[1]
    parts = []
    for j in range(n1):
        w = jnp.zeros((N_KEYS, tt), F32)
        for h in range(H):
            w = w + jnp.where(r2_ref[h] < l1_ref[h, j:j + 1, :], e2_ref[h] * e1_ref[h, j:j + 1, :], 0.0)
        parts.append(w * g[j * N_KEYS:(j + 1) * N_KEYS, :])
    p_ref[...] = jnp.concatenate(parts, axis=0).T.astype(p_ref.dtype)


def _peer_p(hf, u, e1, l1, e2, r2):
    T, D = hf.shape
    E = u.shape[0]
    H = e1.shape[0]
    n1 = SUBLANES
    ec = n1 * N_KEYS
    tt = _tile(T, 256, LANES)
    small = pl.BlockSpec((H, n1, tt), lambda c, i: (0, c, i))
    big = pl.BlockSpec((H, N_KEYS, tt), lambda c, i: (0, 0, i))
    return pl.pallas_call(
        functools.partial(_peer_p_kernel, H=H, n1=n1),
        grid=(E // ec, T // tt),
        in_specs=[pl.BlockSpec((tt, D), lambda c, i: (i, 0)),
                  pl.BlockSpec((ec, D), lambda c, i: (c, 0)),
                  small, small, big, big],
        out_specs=pl.BlockSpec((tt, ec), lambda c, i: (i, c)),
        out_shape=jax.ShapeDtypeStruct((T, E), BF16),
        compiler_params=_params("parallel", "parallel"),
        name="peer_p",
    )(hf, u, e1, l1, e2, r2)


def _peer_layer(x, hf, gate, w):
    B, S, D = x.shape
    T = B * S
    hf2 = hf.reshape(T, D)
    q = _matmul(hf2, w["w_q"], BF16, name="peer_q")
    e1, l1, e2, r2 = _route(q, w["keys"])
    p = _peer_p(hf2, w["u"], e1, l1, e2, r2)
    x_new = _matmul(p, w["v"], F32, res=x.reshape(T, D), gate=gate, seq=S, tm=1024, tn=1024, tk=1024,
                    name="peer_out")
    return x_new.reshape(B, S, D)


def _forward(x, mods, pos_off, Wl, past):
    ckvs, kpes, convs, hss = [], [], [], []
    for i in range(DEPTH):
        sh_m, sc_m, g_m, sh_f, sc_f, g_f = mods[i]
        w = Wl[i]
        hm = _norm_mod(x, w["norm_mix"], sh_m, sc_m)
        j = i // 2
        if i % 2 == 0:
            p = None if past is None else (past["ckv"][j], past["kpe"][j])
            x, ckv, kpe = _mla_layer(x, hm, g_m, pos_off, w["mla"], p)
            ckvs.append(ckv)
            kpes.append(kpe)
        else:
            p = None if past is None else (past["conv"][j], past["h"][j])
            x, conv_new, h_new = _lru_layer(x, hm, g_m, w["lru"], p)
            convs.append(conv_new)
            hss.append(h_new)
        hf = _norm_mod(x, w["norm_ffn"], sh_f, sc_f)
        x = _peer_layer(x, hf, g_f, w["peer"])
    return x, jnp.stack(ckvs), jnp.stack(kpes), jnp.stack(convs), jnp.stack(hss)


def _head_gains(g):
    gn = g[:NOPE_DIM].reshape(1, NOPE_DIM)
    gr = jnp.concatenate([g[NOPE_DIM:], g[NOPE_DIM:]]).reshape(1, ROPE_DIM)
    return gn, gr


def kernel(x_prompt, x_sample, c_prompt, c_sample, cache_ckv, cache_kpe, state_conv, state_h, norm_mix, norm_ffn, w_ada, b_ada, mla_w_in, mla_q_lat_norm, mla_kv_lat_norm, mla_w_uq, mla_w_ukv, mla_q_norm, mla_k_norm, mla_w_o, lru_w_in, lru_conv_w, lru_conv_b, lru_w_a, lru_b_a, lru_w_x, lru_b_x, lru_lambda, lru_w_out, peer_w_q, peer_sub_keys, peer_u, peer_v):
    D = x_prompt.shape[-1]
    nbp = c_prompt.shape[0]
    Wl = []
    for i in range(DEPTH):
        j = i // 2
        w = dict(norm_mix=norm_mix[i], norm_ffn=norm_ffn[i])
        if i % 2 == 0:
            w_in = mla_w_in[j]
            lat = Q_LORA + KV_LORA
            pad = jnp.zeros((D, LANES - ROPE_DIM), F32)
            w_in_ext = jnp.concatenate([w_in, pad, _rot_cols(w_in[:, lat:]), pad], axis=1)
            uq = mla_w_uq[j].reshape(Q_LORA, MLA_HEADS, QK_DIM)
            uq = jnp.concatenate([uq, _rot_cols(uq[..., NOPE_DIM:])], axis=-1).transpose(1, 0, 2)
            ukv = mla_w_ukv[j].reshape(KV_LORA, MLA_HEADS, NOPE_DIM + V_DIM).transpose(1, 0, 2)
            qgn, qgr = _head_gains(mla_q_norm[j])
            kgn, kgr = _head_gains(mla_k_norm[j])
            qgr = jnp.concatenate([qgr, jnp.zeros((1, LANES - ROPE_DIM), F32)], axis=1)
            w["mla"] = dict(w_in=w_in_ext.astype(BF16), q_lat_g=mla_q_lat_norm[j], kv_lat_g=mla_kv_lat_norm[j],
                            w_uq=uq.astype(BF16), w_ukv=ukv.astype(BF16), qg_nope=qgn, qg_rope=qgr,
                            kg_nope=kgn, kg_rope=kgr, w_o=mla_w_o[j].astype(BF16))
        else:
            nblk, bw = lru_b_a.shape[1:]
            w["lru"] = dict(w_in=lru_w_in[j].astype(BF16), conv_w=lru_conv_w[j], conv_b=lru_conv_b[j],
                            w_a=lru_w_a[j].astype(BF16), b_a=lru_b_a[j].reshape(nblk, 1, bw),
                            w_x=lru_w_x[j].astype(BF16), b_x=lru_b_x[j].reshape(nblk, 1, bw),
                            lam=lru_lambda[j], w_out=lru_w_out[j].astype(BF16))
        w["peer"] = dict(w_q=peer_w_q[i].astype(BF16), keys=peer_sub_keys[i].astype(BF16),
                         u=peer_u[i].astype(BF16), v=peer_v[i].astype(BF16))
        Wl.append(w)

    mod = _ada(jnp.concatenate([c_prompt, c_sample], axis=0), w_ada, b_ada)

    def mods_for(lo, hi):
        return [[mod[i, lo:hi, k * D:(k + 1) * D].reshape(hi - lo, 1, D) for k in range(6)] for i in range(DEPTH)]

    y_p, ckv_p, kpe_p, conv_p, h_p = _forward(x_prompt, mods_for(0, nbp), 0, Wl, None)
    past = dict(ckv=cache_ckv, kpe=cache_kpe, conv=state_conv, h=state_h)
    y_s, ckv_s, kpe_s, conv_s, h_s = _forward(x_sample, mods_for(nbp, mod.shape[1]), cache_ckv.shape[2], Wl, past)
    return (y_p, y_s, ckv_p, kpe_p, ckv_s, kpe_s, conv_p, h_p, conv_s, h_s)
```
